```python
import math
import jax, jax.numpy as jnp
from jax import lax
import numpy as np

D_MODEL = 1024
BATCH = 8
SEQ = 4096
DEPTH = 1
DEC_BATCH = 4
DEC_SEQ = 4096
PAST_LEN = 128

GRID_W = 64
RET_HEADS = 4
RET_DK = 128
RET_DV = 128
RET_CHUNK = 128
ROPE_BASE = 10000.0
NA_HEADS = 8
NA_HD = 64
NA_WIN_ROWS = 8
NA_WIN_COLS = 16
N_EXPERTS = 32
TOP_K = 4
D_EXPERT = D_MODEL
SWIGLU_ALPHA = 1.702
SWIGLU_LIMIT = 7.0
MOE_BLOCK = 128
NORM_EPS = 1e-6

RET_W = RET_HEADS * RET_DK
RET_VW = RET_HEADS * RET_DV
NA_W = NA_HEADS * NA_HD
IN_SPLITS = [RET_W, RET_W, RET_VW, RET_VW, NA_W, NA_W, NA_W, D_MODEL, D_MODEL]
D_IN = sum(IN_SPLITS)

kernel_name = "hybrid_retention_natten_moe_encoder"


def rmsnorm(x, g):
    xf = x.astype(jnp.float32)
    y = xf * lax.rsqrt(jnp.mean(xf * xf, axis=-1, keepdims=True) + NORM_EPS)
    return (y * g.astype(jnp.float32)).astype(x.dtype)


def rotary(x, pos):
    d = x.shape[-1]
    half = d // 2
    inv = 1.0 / (ROPE_BASE ** (jnp.arange(half, dtype=jnp.float32) * 2.0 / d))
    ang = pos[:, None] * inv[None, :]
    cos, sin = jnp.cos(ang), jnp.sin(ang)
    x1, x2 = x[..., :half], x[..., half:]
    return jnp.concatenate([x1 * cos - x2 * sin, x1 * sin + x2 * cos], axis=-1)


def retention_direction(q, k, v, log_g, strict):
    B, H, T, dk = q.shape
    dv = v.shape[-1]
    C = RET_CHUNK
    N = T // C
    qc = q.reshape(B, H, N, C, dk)
    kc = k.reshape(B, H, N, C, dk)
    vc = v.reshape(B, H, N, C, dv)
    j = jnp.arange(C, dtype=jnp.float32)
    diff = j[:, None] - j[None, :]
    mask = diff > 0 if strict else diff >= 0
    lg = log_g[:, None, None]
    dmat = jnp.where(mask[None], jnp.exp(lg * jnp.where(mask, diff, 0.0)[None]), 0.0)
    scores = jnp.einsum('bhncd,bhnsd->bhncs', qc, kc) * dmat[None, :, None]
    inner = jnp.einsum('bhncs,bhnse->bhnce', scores, vc)
    kdec = jnp.exp(log_g[:, None] * (C - 1.0 - j)[None, :])
    kv = jnp.einsum('bhnsd,hs,bhnse->bhnde', kc, kdec, vc)
    chunk_decay = jnp.exp(log_g * C)[None, :, None, None]

    def step(S, kv_n):
        return chunk_decay * S + kv_n, S

    _, prev = lax.scan(step, jnp.zeros((B, H, dk, dv), jnp.float32), jnp.moveaxis(kv, 2, 0))
    qdec = jnp.exp(log_g[:, None] * (j + 1.0)[None, :])
    cross = jnp.einsum('bhncd,hc,nbhde->bhnce', qc, qdec, prev)
    return (inner + cross).reshape(B, H, T, dv)


def neighborhood_attention(q, k, v, rpb):
    B, T, _ = q.shape
    rows = T // GRID_W
    wr = min(NA_WIN_ROWS, rows)

    def to_grid(a):
        return a.reshape(B, rows, GRID_W, NA_HEADS, NA_HD).transpose(0, 3, 1, 2, 4)

    qg, kg, vg = to_grid(q), to_grid(k), to_grid(v)
    cols = np.arange(GRID_W)
    cs = np.clip(cols - NA_WIN_COLS // 2, 0, GRID_W - NA_WIN_COLS)
    col_idx = cs[:, None] + np.arange(NA_WIN_COLS)[None, :]
    dc = col_idx - cols[:, None] + NA_WIN_COLS - 1
    rpb_c = rpb[:, :, dc]
    scale = NA_HD ** -0.5

    def row_block(r):
        rs = jnp.clip(r - wr // 2, 0, rows - wr)
        q_r = lax.dynamic_index_in_dim(qg, r, axis=2, keepdims=False)
        k_win = lax.dynamic_slice_in_dim(kg, rs, wr, axis=2)[:, :, :, col_idx]
        v_win = lax.dynamic_slice_in_dim(vg, rs, wr, axis=2)[:, :, :, col_idx]
        s = jnp.einsum('bhqd,bhrqcd->bhqrc', q_r, k_win).astype(jnp.float32) * scale
        dr = rs + jnp.arange(wr) - r + NA_WIN_ROWS - 1
        bias = rpb_c[:, dr].transpose(0, 2, 1, 3).astype(jnp.float32)
        s = s + bias[None]
        p = jax.nn.softmax(s.reshape(B, NA_HEADS, GRID_W, wr * NA_WIN_COLS), axis=-1).reshape(s.shape)
        return jnp.einsum('bhqrc,bhrqcd->bhqd', p.astype(v_win.dtype), v_win)

    o = lax.map(row_block, jnp.arange(rows))
    return o.transpose(1, 0, 3, 2, 4).reshape(B, T, NA_W)


def mixer(h, w_in, decay_logit, rpb, w_ret_out, w_na_out, w_o):
    B, T, _ = h.shape
    proj = h @ w_in
    rq, rk, rv, rg, nq, nk, nv, g_ret, g_na = jnp.split(proj, [int(c) for c in np.cumsum(IN_SPLITS)[:-1]], axis=-1)

    def heads(a, d):
        return a.reshape(B, T, RET_HEADS, d).transpose(0, 2, 1, 3).astype(jnp.float32)

    pos = jnp.arange(T, dtype=jnp.float32)
    q = rotary(heads(rq, RET_DK), pos)
    k = rotary(heads(rk, RET_DK), pos) * (RET_DK ** -0.5)
    v = heads(rv, RET_DV)
    log_g = jax.nn.log_sigmoid(decay_logit.astype(jnp.float32))
    fwd = retention_direction(q, k, v, log_g[0], False)
    bwd = retention_direction(q[:, :, ::-1], k[:, :, ::-1], v[:, :, ::-1], log_g[1], True)[:, :, ::-1]
    o = fwd + bwd
    mu = jnp.mean(o, axis=-1, keepdims=True)
    var = jnp.mean(jnp.square(o - mu), axis=-1, keepdims=True)
    o = (o - mu) * lax.rsqrt(var + NORM_EPS)
    o = o.transpose(0, 2, 1, 3).reshape(B, T, RET_VW).astype(h.dtype)
    ret_out = (jax.nn.silu(rg) * o) @ w_ret_out
    na_out = neighborhood_attention(nq, nk, nv, rpb) @ w_na_out
    merged = jax.nn.sigmoid(g_ret) * ret_out + jax.nn.sigmoid(g_na) * na_out
    return merged @ w_o


def moe(h, w_router, b_router, w_gate_up, b_gate_up, w_down, b_down):
    B, T, D = h.shape
    N = B * T
    xt = h.reshape(N, D)
    logits = (xt @ w_router + b_router).astype(jnp.float32)
    top_v, top_e = lax.top_k(logits, TOP_K)
    gates = jax.nn.softmax(top_v, axis=-1)
    A = N * TOP_K
    e_flat = top_e.reshape(A)
    tok_flat = jnp.arange(A) // TOP_K
    order = jnp.argsort(e_flat)
    e_sorted = e_flat[order]
    tok_sorted = tok_flat[order]
    g_sorted = gates.reshape(A)[order]
    counts = jnp.zeros((N_EXPERTS,), jnp.int32).at[e_flat].add(1)
    padded = (counts + MOE_BLOCK - 1) // MOE_BLOCK * MOE_BLOCK
    start = jnp.cumsum(counts) - counts
    pend = jnp.cumsum(padded)
    pstart = pend - padded
    dest = pstart[e_sorted] + (jnp.arange(A) - start[e_sorted])
    P = A + N_EXPERTS * MOE_BLOCK
    nb = P // MOE_BLOCK
    x_buf = jnp.zeros((P, D), h.dtype).at[dest].set(xt[tok_sorted])
    blk_e = jnp.minimum(jnp.searchsorted(pend, jnp.arange(nb) * MOE_BLOCK, side='right'), N_EXPERTS - 1)

    def expert_block(args):
        xb, e = args
        gu = xb @ w_gate_up[e] + b_gate_up[e]
        gate, up = gu[:, :D_EXPERT], gu[:, D_EXPERT:]
        gate = jnp.minimum(gate, SWIGLU_LIMIT)
        up = jnp.clip(up, -SWIGLU_LIMIT, SWIGLU_LIMIT)
        act = (up + 1.0) * (gate * jax.nn.sigmoid(SWIGLU_ALPHA * gate))
        return act @ w_down[e] + b_down[e]

    y_buf = lax.map(expert_block, (x_buf.reshape(nb, MOE_BLOCK, D), blk_e)).reshape(P, D)
    y_assign = y_buf[dest].astype(jnp.float32) * g_sorted[:, None]
    out = jax.ops.segment_sum(y_assign, tok_sorted, num_segments=N)
    return out.reshape(B, T, D).astype(h.dtype)


def trunk(x, norm_mix, w_in, decay_logit, na_rpb, w_ret_out, w_na_out, w_o, norm_ffn,
          w_router, b_router, w_gate_up, b_gate_up, w_down, b_down, norm_final):
    for l in range(DEPTH):
        x = x + mixer(rmsnorm(x, norm_mix[l]), w_in[l], decay_logit[l], na_rpb[l],
                      w_ret_out[l], w_na_out[l], w_o[l])
        x = x + moe(rmsnorm(x, norm_ffn[l]), w_router[l], b_router[l], w_gate_up[l],
                    b_gate_up[l], w_down[l], b_down[l])
    return rmsnorm(x, norm_final)


def setup_inputs(seed: int = 0) -> dict:
    key = jax.random.key(seed)
    ks = jax.random.split(key, 20)
    f32 = jnp.float32
    nrm = lambda k, shape, s: jax.random.normal(k, shape, f32) * s
    base_g = 1.0 - 2.0 ** (-5.0 - np.arange(RET_HEADS, dtype=np.float32))
    base_logit = jnp.asarray(np.log(base_g / (1.0 - base_g)).astype(np.float32))
    return {
        "x_prompt": nrm(ks[0], (BATCH, SEQ, D_MODEL), 1.0),
        "x_sample": nrm(ks[1], (DEC_BATCH, DEC_SEQ, D_MODEL), 1.0),
        "norm_mix": 1.0 + nrm(ks[2], (DEPTH, D_MODEL), 0.02),
        "w_in": nrm(ks[3], (DEPTH, D_MODEL, D_IN), D_MODEL ** -0.5),
        "decay_logit": base_logit[None, None, :] + nrm(ks[4], (DEPTH, 2, RET_HEADS), 0.1),
        "na_rpb": nrm(ks[5], (DEPTH, NA_HEADS, 2 * NA_WIN_ROWS - 1, 2 * NA_WIN_COLS - 1), 0.05),
        "w_ret_out": nrm(ks[6], (DEPTH, RET_VW, D_MODEL), RET_VW ** -0.5),
        "w_na_out": nrm(ks[7], (DEPTH, NA_W, D_MODEL), NA_W ** -0.5),
        "w_o": nrm(ks[8], (DEPTH, D_MODEL, D_MODEL), D_MODEL ** -0.5),
        "norm_ffn": 1.0 + nrm(ks[9], (DEPTH, D_MODEL), 0.02),
        "w_router": nrm(ks[10], (DEPTH, D_MODEL, N_EXPERTS), D_MODEL ** -0.5),
        "b_router": nrm(ks[11], (DEPTH, N_EXPERTS), 0.01),
        "w_gate_up": nrm(ks[12], (DEPTH, N_EXPERTS, D_MODEL, 2 * D_EXPERT), D_MODEL ** -0.5),
        "b_gate_up": nrm(ks[13], (DEPTH, N_EXPERTS, 2 * D_EXPERT), 0.01),
        "w_down": nrm(ks[14], (DEPTH, N_EXPERTS, D_EXPERT, D_MODEL), D_EXPERT ** -0.5),
        "b_down": nrm(ks[15], (DEPTH, N_EXPERTS, D_MODEL), 0.01),
        "norm_final": 1.0 + nrm(ks[16], (D_MODEL,), 0.02),
    }


def reference(x_prompt, x_sample, norm_mix, w_in, decay_logit, na_rpb, w_ret_out, w_na_out, w_o,
              norm_ffn, w_router, b_router, w_gate_up, b_gate_up, w_down, b_down, norm_final):
    y_prompt = trunk(x_prompt, norm_mix, w_in, decay_logit, na_rpb, w_ret_out, w_na_out, w_o, norm_ffn,
                     w_router, b_router, w_gate_up, b_gate_up, w_down, b_down, norm_final)
    y_sample = trunk(x_sample, norm_mix, w_in, decay_logit, na_rpb, w_ret_out, w_na_out, w_o, norm_ffn,
                     w_router, b_router, w_gate_up, b_gate_up, w_down, b_down, norm_final)
    return (y_prompt, y_sample)
```

```python
import functools

import numpy as np
import jax
import jax.numpy as jnp
from jax import lax
from jax.experimental import pallas as pl
from jax.experimental.pallas import tpu as pltpu

F32 = jnp.float32
BF16 = jnp.bfloat16

D_MODEL = 1024
GRID_W = 64
RET_HEADS = 4
RET_DK = 128
RET_CHUNK = 128
ROPE_BASE = 10000.0
NA_HEADS = 8
NA_HD = 64
NA_WIN_ROWS = 8
NA_WIN_COLS = 16
N_EXPERTS = 32
TOP_K = 4
D_EXPERT = 1024
SWIGLU_ALPHA = 1.702
SWIGLU_LIMIT = 7.0
NORM_EPS = 1e-6
RET_W = RET_HEADS * RET_DK
NA_W = NA_HEADS * NA_HD
D_IN = 4 * RET_W + 3 * NA_W + 2 * D_MODEL

LANES = 128
VMEM_LIMIT = 56 * 1024 * 1024
TM_PROJ = 512
TM_DISPATCH = 512
TM_COMBINE = 256
BM_EXPERT = 512
CH_EXPERT = 512
NEG_MASK = -1e30


def _const_spec(shape):
    nd = len(shape)
    return pl.BlockSpec(shape, lambda *_: (0,) * nd, pipeline_mode=pl.Buffered(1))


def _rms(x, g):
    ms = jnp.mean(x * x, axis=-1, keepdims=True)
    return x * lax.rsqrt(ms + NORM_EPS) * g


def _in_proj_kernel(x_ref, g_ref, w_ref, cos_ref, sin_ref,
                    q_ref, k_ref, v_ref, gate_ref, nq_ref, nk_ref, nv_ref, sret_ref, sna_ref):
    h = _rms(x_ref[...], g_ref[...]).astype(BF16)
    cos = cos_ref[...]
    sin = sin_ref[...]

    def proj(lo, width):
        return jnp.dot(h, w_ref[:, lo:lo + width], preferred_element_type=F32)

    def rot(dst, a, scale):
        for hd in range(RET_HEADS):
            blk = a[:, hd * RET_DK:(hd + 1) * RET_DK]
            r = blk * cos + pltpu.roll(blk, RET_DK // 2, 1) * sin
            if scale is not None:
                r = r * scale
            dst[:, hd * RET_DK:(hd + 1) * RET_DK] = r.astype(BF16)

    o = 0
    rot(q_ref, proj(o, RET_W), None); o += RET_W
    rot(k_ref, proj(o, RET_W), RET_DK ** -0.5); o += RET_W
    v_ref[...] = proj(o, RET_W).astype(BF16); o += RET_W
    rg = proj(o, RET_W); o += RET_W
    gate_ref[...] = (rg * jax.nn.sigmoid(rg)).astype(BF16)
    nq_ref[...] = (proj(o, NA_W) * (NA_HD ** -0.5)).astype(BF16); o += NA_W
    nk_ref[...] = proj(o, NA_W).astype(BF16); o += NA_W
    nv_ref[...] = proj(o, NA_W).astype(BF16); o += NA_W
    sret_ref[...] = jax.nn.sigmoid(proj(o, D_MODEL)).astype(BF16); o += D_MODEL
    sna_ref[...] = jax.nn.sigmoid(proj(o, D_MODEL)).astype(BF16)


def _in_proj(x2, g, w_in, cos_t, sin_t, T):
    N = x2.shape[0]
    tm = TM_PROJ
    nt = T // tm
    row = lambda w: pl.BlockSpec((tm, w), lambda i: (i, 0))
    pos = pl.BlockSpec((tm, RET_DK), lambda i: (i % nt, 0))
    half = jax.ShapeDtypeStruct((N, RET_W), BF16)
    full = jax.ShapeDtypeStruct((N, D_MODEL), BF16)
    return pl.pallas_call(
        _in_proj_kernel,
        grid=(N // tm,),
        in_specs=[row(D_MODEL), _const_spec((1, D_MODEL)), _const_spec((D_MODEL, D_IN)), pos, pos],
        out_specs=[row(RET_W)] * 7 + [row(D_MODEL)] * 2,
        out_shape=[half] * 7 + [full] * 2,
        compiler_params=pltpu.CompilerParams(dimension_semantics=("parallel",), vmem_limit_bytes=VMEM_LIMIT),
        name="in_proj",
    )(x2, g, w_in, cos_t, sin_t)


def _retention_kernel(lg_ref, q_ref, k_ref, v_ref, gate_ref, o_ref, acc_ref, *, T):
    C = RET_CHUNK
    nchunk = T // C
    hd = pl.program_id(1)
    lgf = lg_ref[0, hd]
    lgb = lg_ref[1, hd]
    ii = lax.broadcasted_iota(jnp.int32, (C, C), 0).astype(F32)
    jj = lax.broadcasted_iota(jnp.int32, (C, C), 1).astype(F32)
    diff = ii - jj
    dmat_f = jnp.where(diff >= 0, jnp.exp(lgf * jnp.maximum(diff, 0.0)), 0.0)
    dmat_b = jnp.where(diff < 0, jnp.exp(lgb * jnp.maximum(-diff, 0.0)), 0.0)
    col = lax.broadcasted_iota(jnp.int32, (C, 1), 0).astype(F32)
    qdec_f = jnp.exp(lgf * (col + 1.0))
    kdec_f = jnp.exp(lgf * (C - 1.0 - col))
    qdec_b = jnp.exp(lgb * (C - col))
    kdec_b = jnp.exp(lgb * col)
    ones = jnp.ones((1, RET_DK), F32)
    cdec_f = jnp.exp(ones * (lgf * C))
    cdec_b = jnp.exp(ones * (lgb * C))
    nt_dims = (((1,), (1,)), ((), ()))
    tn_dims = (((0,), (0,)), ((), ()))

    def chunk(n, state, dmat, qdec, kdec, cdec):
        sl = pl.ds(pl.multiple_of(n * C, C), C)
        q = q_ref[sl, :]
        k = k_ref[sl, :]
        v = v_ref[sl, :]
        s = lax.dot_general(q, k, nt_dims, preferred_element_type=F32) * dmat
        inner = jnp.dot(s.astype(BF16), v, preferred_element_type=F32)
        qd = (q.astype(F32) * qdec).astype(BF16)
        cross = jnp.dot(qd, state.astype(BF16), preferred_element_type=F32)
        kd = (k.astype(F32) * kdec).astype(BF16)
        kv = lax.dot_general(kd, v, tn_dims, preferred_element_type=F32)
        return sl, inner + cross, cdec * state + kv

    def fwd(n, state):
        sl, o, state = chunk(n, state, dmat_f, qdec_f, kdec_f, cdec_f)
        acc_ref[sl, :] = o
        return state

    def bwd(i, state):
        sl, o, state = chunk(nchunk - 1 - i, state, dmat_b, qdec_b, kdec_b, cdec_b)
        o = o + acc_ref[sl, :]
        mu = jnp.mean(o, axis=-1, keepdims=True)
        d = o - mu
        var = jnp.mean(d * d, axis=-1, keepdims=True)
        o_ref[sl, :] = (gate_ref[sl, :].astype(F32) * (d * lax.rsqrt(var + NORM_EPS))).astype(BF16)
        return state

    zero = jnp.zeros((RET_DK, RET_DK), F32)
    lax.fori_loop(0, nchunk, fwd, zero)
    lax.fori_loop(0, nchunk, bwd, zero)


def _retention(log_g, q, k, v, gate, B, T):
    N = q.shape[0]
    seq = pl.BlockSpec((T, RET_DK), lambda b, h: (b, h))
    return pl.pallas_call(
        functools.partial(_retention_kernel, T=T),
        grid=(B, RET_HEADS),
        in_specs=[pl.BlockSpec(memory_space=pltpu.SMEM), seq, seq, seq, seq],
        out_specs=seq,
        out_shape=jax.ShapeDtypeStruct((N, RET_W), BF16),
        scratch_shapes=[pltpu.VMEM((T, RET_DK), F32)],
        compiler_params=pltpu.CompilerParams(dimension_semantics=("parallel", "parallel"), vmem_limit_bytes=VMEM_LIMIT),
        name="retention",
    )(log_g, q, k, v, gate)


def _na_bias_table(rpb):
    W, WC, R = GRID_W, NA_WIN_COLS, NA_WIN_ROWS
    c = np.arange(W)
    cs = np.clip(c - WC // 2, 0, W - WC)
    kc = np.arange(W)
    valid = (kc[None, :] >= cs[:, None]) & (kc[None, :] < cs[:, None] + WC)
    dc = np.clip(kc[None, :] - c[:, None] + WC - 1, 0, 2 * WC - 2)
    bc = jnp.where(valid[None, None], rpb[:, :, dc], NEG_MASK)
    dr = np.arange(R)[:, None] + np.arange(R)[None, :]
    t = bc[:, dr]
    return t.transpose(0, 1, 3, 2, 4).reshape(NA_HEADS, R, W, R * W).astype(F32)


def _natten_kernel(q_ref, k_ref, v_ref, bias_ref, o_ref, *, T):
    W, R = GRID_W, NA_WIN_ROWS
    rows = T // W
    lane = lax.broadcasted_iota(jnp.int32, (1, 2 * NA_HD), 1)
    masks = [(lane < NA_HD).astype(BF16), (lane >= NA_HD).astype(BF16)]
    nt_dims = (((1,), (1,)), ((), ()))

    def body(r, carry):
        rs = jnp.clip(r - R // 2, 0, rows - R)
        d0 = rs - r + R - 1
        qs = pl.ds(pl.multiple_of(r * W, W), W)
        ks = pl.ds(pl.multiple_of(rs * W, W), R * W)
        q = q_ref[qs, :]
        kw = k_ref[ks, :]
        vw = v_ref[ks, :]
        out = jnp.zeros((W, 2 * NA_HD), F32)
        for hh in range(2):
            s = lax.dot_general(q * masks[hh], kw, nt_dims, preferred_element_type=F32) + bias_ref[hh, d0]
            p = jnp.exp(s - jnp.max(s, axis=-1, keepdims=True))
            l = jnp.sum(p, axis=-1, keepdims=True)
            pv = jnp.dot(p.astype(BF16), vw * masks[hh], preferred_element_type=F32)
            out = out + pv / l
        o_ref[qs, :] = out.astype(BF16)
        return carry

    lax.fori_loop(0, rows, body, 0)


def _natten(nq, nk, nv, bias_tab, B, T):
    N = nq.shape[0]
    seq = pl.BlockSpec((T, 2 * NA_HD), lambda b, h: (b, h))
    bias = pl.BlockSpec((2, NA_WIN_ROWS, GRID_W, NA_WIN_ROWS * GRID_W), lambda b, h: (h, 0, 0, 0))
    return pl.pallas_call(
        functools.partial(_natten_kernel, T=T),
        grid=(B, NA_HEADS // 2),
        in_specs=[seq, seq, seq, bias],
        out_specs=seq,
        out_shape=jax.ShapeDtypeStruct((N, NA_W), BF16),
        compiler_params=pltpu.CompilerParams(dimension_semantics=("parallel", "parallel"), vmem_limit_bytes=VMEM_LIMIT),
        name="natten",
    )(nq, nk, nv, bias_tab)


def _post_mixer_kernel(ret_ref, na_ref, sret_ref, sna_ref, x_ref, wret_ref, wna_ref, wo_ref, g_ref,
                       wrh_ref, wrl_ref, br_ref,
                       x1_ref, h2_ref, e_ref, gates_ref, rank_ref, cnt_ref, carry_ref):
    tm = x_ref.shape[0]

    @pl.when(pl.program_id(0) == 0)
    def _():
        carry_ref[...] = jnp.zeros_like(carry_ref)

    ret_out = jnp.dot(ret_ref[...], wret_ref[...], preferred_element_type=F32)
    na_out = jnp.dot(na_ref[...], wna_ref[...], preferred_element_type=F32)
    merged = sret_ref[...].astype(F32) * ret_out + sna_ref[...].astype(F32) * na_out
    x1 = x_ref[...] + jnp.dot(merged.astype(BF16), wo_ref[...], preferred_element_type=F32)
    x1_ref[...] = x1
    h2 = _rms(x1, g_ref[...])
    h2_ref[...] = h2

    nt_dims = (((1,), (1,)), ((), ()))
    h_hi = h2.astype(BF16)
    h_lo = (h2 - h_hi.astype(F32)).astype(BF16)
    logits = (lax.dot_general(wrh_ref[...], h_hi, nt_dims, preferred_element_type=F32)
              + lax.dot_general(wrl_ref[...], h_hi, nt_dims, preferred_element_type=F32)
              + lax.dot_general(wrh_ref[...], h_lo, nt_dims, preferred_element_type=F32)
              + br_ref[...])

    eid = lax.broadcasted_iota(jnp.int32, (N_EXPERTS, tm), 0).astype(F32)
    work = logits
    sel, val = [], []
    for _ in range(TOP_K):
        m = jnp.max(work, axis=0, keepdims=True)
        idx = jnp.min(jnp.where(work == m, eid, float(N_EXPERTS)), axis=0, keepdims=True)
        sel.append(idx)
        val.append(m)
        work = jnp.where(eid == idx, -jnp.inf, work)
    ex = [jnp.exp(v - val[0]) for v in val]
    den = ex[0] + ex[1] + ex[2] + ex[3]

    hot = [(eid == s) for s in sel]
    onehot = jnp.where(hot[0] | hot[1] | hot[2] | hot[3], 1.0, 0.0)
    tri = jnp.where(lax.broadcasted_iota(jnp.int32, (tm, tm), 0) < lax.broadcasted_iota(jnp.int32, (tm, tm), 1),
                    1.0, 0.0)
    prefix = jnp.dot(onehot.astype(BF16), tri.astype(BF16), preferred_element_type=F32)
    base = prefix + carry_ref[:, 0:1]
    for k in range(TOP_K):
        e_ref[k:k + 1, :] = sel[k].astype(jnp.int32)
        gates_ref[k:k + 1, :] = ex[k] / den
        rank_ref[k:k + 1, :] = jnp.sum(jnp.where(hot[k], base, 0.0), axis=0, keepdims=True).astype(jnp.int32)
    carry_ref[...] = carry_ref[...] + jnp.sum(onehot, axis=1, keepdims=True)
    cnt_ref[...] = carry_ref[...].astype(jnp.int32)


def _post_mixer(ret, na, sret, sna, x2, wret, wna, wo, g, wr_hi, wr_lo, br):
    N = x2.shape[0]
    tm = TM_PROJ
    row = lambda w: pl.BlockSpec((tm, w), lambda i: (i, 0))
    tok = pl.BlockSpec((TOP_K, tm), lambda i: (0, i))
    return pl.pallas_call(
        _post_mixer_kernel,
        grid=(N // tm,),
        in_specs=[row(RET_W), row(NA_W), row(D_MODEL), row(D_MODEL), row(D_MODEL),
                  _const_spec((RET_W, D_MODEL)), _const_spec((NA_W, D_MODEL)), _const_spec((D_MODEL, D_MODEL)),
                  _const_spec((1, D_MODEL)), _const_spec((N_EXPERTS, D_MODEL)), _const_spec((N_EXPERTS, D_MODEL)),
                  _const_spec((N_EXPERTS, 1))],
        out_specs=[row(D_MODEL), row(D_MODEL), tok, tok, tok,
                   pl.BlockSpec((N_EXPERTS, LANES), lambda i: (0, 0))],
        out_shape=[jax.ShapeDtypeStruct((N, D_MODEL), F32), jax.ShapeDtypeStruct((N, D_MODEL), F32),
                   jax.ShapeDtypeStruct((TOP_K, N), jnp.int32), jax.ShapeDtypeStruct((TOP_K, N), F32),
                   jax.ShapeDtypeStruct((TOP_K, N), jnp.int32),
                   jax.ShapeDtypeStruct((N_EXPERTS, LANES), jnp.int32)],
        scratch_shapes=[pltpu.VMEM((N_EXPERTS, LANES), F32)],
        compiler_params=pltpu.CompilerParams(dimension_semantics=("arbitrary",), vmem_limit_bytes=VMEM_LIMIT),
        name="post_mixer",
    )(ret, na, sret, sna, x2, wret, wna, wo, g, wr_hi, wr_lo, br)


def _row_copy(src, s, dst, d, sem):
    return pltpu.make_async_copy(src.at[pl.ds(s, 1), :], dst.at[pl.ds(d, 1), :], sem)


def _dispatch_kernel(zstart_ref, znum_ref, dest_ref, h_ref, xbuf_ref, zero_ref, sem, zsem):
    tm = h_ref.shape[0]

    @pl.when(pl.program_id(0) == 0)
    def _():
        zero_ref[...] = jnp.zeros_like(zero_ref)

        def per_expert(e, c):
            base = zstart_ref[e]

            def fill(j, c2):
                _row_copy(zero_ref, 0, xbuf_ref, base + j, zsem).start()
                return c2

            lax.fori_loop(0, znum_ref[e], fill, 0)

            def drain(j, c2):
                _row_copy(zero_ref, 0, xbuf_ref, base + j, zsem).wait()
                return c2

            lax.fori_loop(0, znum_ref[e], drain, 0)
            return c

        lax.fori_loop(0, N_EXPERTS, per_expert, 0)

    def issue(j, c):
        for k in range(TOP_K):
            _row_copy(h_ref, j, xbuf_ref, dest_ref[0, k * tm + j], sem).start()
        return c

    lax.fori_loop(0, tm, issue, 0)

    def drain(j, c):
        for k in range(TOP_K):
            _row_copy(h_ref, j, xbuf_ref, dest_ref[0, k * tm + j], sem).wait()
        return c

    lax.fori_loop(0, tm, drain, 0)


def _dispatch(zstart, znum, dest_blk, h2, P):
    N = h2.shape[0]
    tm = TM_DISPATCH
    grid_spec = pltpu.PrefetchScalarGridSpec(
        num_scalar_prefetch=2,
        grid=(N // tm,),
        in_specs=[pl.BlockSpec((None, 1, TOP_K * tm), lambda i, *_: (i, 0, 0), memory_space=pltpu.SMEM),
                  pl.BlockSpec((tm, D_MODEL), lambda i, *_: (i, 0))],
        out_specs=pl.BlockSpec(memory_space=pl.ANY),
        scratch_shapes=[pltpu.VMEM((8, D_MODEL), F32), pltpu.SemaphoreType.DMA, pltpu.SemaphoreType.DMA],
    )
    return pl.pallas_call(
        _dispatch_kernel,
        grid_spec=grid_spec,
        out_shape=jax.ShapeDtypeStruct((P, D_MODEL), F32),
        compiler_params=pltpu.CompilerParams(dimension_semantics=("arbitrary",), vmem_limit_bytes=VMEM_LIMIT),
        name="dispatch",
    )(zstart, znum, dest_blk, h2)


def _experts_kernel(be_ref, nu_ref, x_ref, wgu_ref, bgu_ref, wdn_ref, bdn_ref, y_ref, wgu_bf, wdn_bf):
    b = pl.program_id(0)

    @pl.when(b < nu_ref[0])
    def _():
        prev = be_ref[jnp.maximum(b - 1, 0)]

        @pl.when((b == 0) | (be_ref[b] != prev))
        def _():
            wgu_bf[...] = wgu_ref[...].astype(BF16)
            wdn_bf[...] = wdn_ref[...].astype(BF16)

        x = x_ref[...].astype(BF16)
        acc = jnp.zeros(y_ref.shape, F32) + bdn_ref[...]
        for c in range(0, D_EXPERT, CH_EXPERT):
            gate = jnp.dot(x, wgu_bf[:, c:c + CH_EXPERT], preferred_element_type=F32) + bgu_ref[:, c:c + CH_EXPERT]
            up = (jnp.dot(x, wgu_bf[:, D_EXPERT + c:D_EXPERT + c + CH_EXPERT], preferred_element_type=F32)
                  + bgu_ref[:, D_EXPERT + c:D_EXPERT + c + CH_EXPERT])
            gate = jnp.minimum(gate, SWIGLU_LIMIT)
            up = jnp.clip(up, -SWIGLU_LIMIT, SWIGLU_LIMIT)
            act = (up + 1.0) * (gate * jax.nn.sigmoid(SWIGLU_ALPHA * gate))
            acc = acc + jnp.dot(act.astype(BF16), wdn_bf[c:c + CH_EXPERT, :], preferred_element_type=F32)
        y_ref[...] = acc


def _experts(blk_e, n_used, xbuf, wgu, bgu, wdn, bdn, nb):
    bm = BM_EXPERT
    P = xbuf.shape[0]
    used = lambda b, be, nu: jnp.minimum(b, nu[0] - 1)
    grid_spec = pltpu.PrefetchScalarGridSpec(
        num_scalar_prefetch=2,
        grid=(nb,),
        in_specs=[pl.BlockSpec((bm, D_MODEL), lambda b, be, nu: (used(b, be, nu), 0)),
                  pl.BlockSpec((None, D_MODEL, 2 * D_EXPERT), lambda b, be, nu: (be[b], 0, 0)),
                  pl.BlockSpec((None, 1, 2 * D_EXPERT), lambda b, be, nu: (be[b], 0, 0)),
                  pl.BlockSpec((None, D_EXPERT, D_MODEL), lambda b, be, nu: (be[b], 0, 0)),
                  pl.BlockSpec((None, 1, D_MODEL), lambda b, be, nu: (be[b], 0, 0))],
        out_specs=pl.BlockSpec((bm, D_MODEL), lambda b, be, nu: (used(b, be, nu), 0)),
        scratch_shapes=[pltpu.VMEM((D_MODEL, 2 * D_EXPERT), BF16), pltpu.VMEM((D_EXPERT, D_MODEL), BF16)],
    )
    return pl.pallas_call(
        _experts_kernel,
        grid_spec=grid_spec,
        out_shape=jax.ShapeDtypeStruct((P, D_MODEL), F32),
        compiler_params=pltpu.CompilerParams(dimension_semantics=("arbitrary",), vmem_limit_bytes=VMEM_LIMIT),
        name="experts",
    )(blk_e, n_used, xbuf, wgu, bgu, wdn, bdn)


def _combine_kernel(dcur_ref, dnxt_ref, gates_ref, x1_ref, g_ref, ybuf_ref, o_ref, gbuf, sems):
    tm = x1_ref.shape[0]
    i = pl.program_id(0)
    n = pl.num_programs(0)
    slot = lax.rem(i, 2)

    def gather(dref, s, start):
        def body(j, c):
            for k in range(TOP_K):
                cp = pltpu.make_async_copy(ybuf_ref.at[pl.ds(dref[0, k * tm + j], 1), :],
                                           gbuf.at[s, k, pl.ds(j, 1), :], sems.at[s])
                if start:
                    cp.start()
                else:
                    cp.wait()
            return c

        lax.fori_loop(0, tm, body, 0)

    @pl.when(i == 0)
    def _():
        gather(dcur_ref, 0, True)

    @pl.when(i + 1 < n)
    def _():
        gather(dnxt_ref, 1 - slot, True)

    gather(dcur_ref, slot, False)

    out = x1_ref[...]
    gt = gates_ref[...]
    for k in range(TOP_K):
        out = out + gt[:, k:k + 1] * gbuf[slot, k]
    o_ref[...] = _rms(out, g_ref[...])


def _combine(dest_blk, gates_t, x1, g, ybuf):
    N = x1.shape[0]
    tm = TM_COMBINE
    nblk = N // tm
    dspec = lambda f: pl.BlockSpec((None, 1, TOP_K * tm), f, memory_space=pltpu.SMEM)
    return pl.pallas_call(
        _combine_kernel,
        grid=(nblk,),
        in_specs=[dspec(lambda i: (i, 0, 0)), dspec(lambda i: (jnp.minimum(i + 1, nblk - 1), 0, 0)),
                  pl.BlockSpec((tm, TOP_K), lambda i: (i, 0)),
                  pl.BlockSpec((tm, D_MODEL), lambda i: (i, 0)),
                  _const_spec((1, D_MODEL)),
                  pl.BlockSpec(memory_space=pl.ANY)],
        out_specs=pl.BlockSpec((tm, D_MODEL), lambda i: (i, 0)),
        out_shape=jax.ShapeDtypeStruct((N, D_MODEL), F32),
        scratch_shapes=[pltpu.VMEM((2, TOP_K, tm, D_MODEL), F32), pltpu.SemaphoreType.DMA((2,))],
        compiler_params=pltpu.CompilerParams(dimension_semantics=("arbitrary",), vmem_limit_bytes=VMEM_LIMIT),
        name="combine",
    )(dest_blk, dest_blk, gates_t, x1, g, ybuf)


def _blocked(dest, tm):
    K, N = dest.shape
    return dest.reshape(K, N // tm, tm).transpose(1, 0, 2).reshape(N // tm, 1, K * tm)


def _trunk(x, p):
    B, T, D = x.shape
    N = B * T
    x2 = x.reshape(N, D)
    q, k, v, gate, nq, nk, nv, sret, sna = _in_proj(x2, p["norm_mix"], p["w_in"], p["cos"], p["sin"], T)
    ret = _retention(p["log_g"], q, k, v, gate, B, T)
    na = _natten(nq, nk, nv, p["na_bias"], B, T)
    x1, h2, e4, g4, r4, cnt = _post_mixer(ret, na, sret, sna, x2, p["w_ret_out"], p["w_na_out"], p["w_o"],
                                          p["norm_ffn"], p["wr_hi"], p["wr_lo"], p["b_router"])
    bm = BM_EXPERT
    counts = cnt[:, 0]
    padded = (counts + bm - 1) // bm * bm
    pend = jnp.cumsum(padded)
    pstart = pend - padded
    dest = pstart[e4] + r4
    nb = (N * TOP_K) // bm + N_EXPERTS
    n_used = (pend[-1] // bm).astype(jnp.int32).reshape(1)
    blk = jnp.minimum(jnp.arange(nb, dtype=jnp.int32), n_used[0] - 1)
    blk_e = jnp.minimum(jnp.searchsorted(pend, blk * bm, side="right"), N_EXPERTS - 1).astype(jnp.int32)
    xbuf = _dispatch((pstart + counts).astype(jnp.int32), (padded - counts).astype(jnp.int32),
                     _blocked(dest, TM_DISPATCH), h2, nb * bm)
    ybuf = _experts(blk_e, n_used, xbuf, p["w_gate_up"], p["b_gate_up"], p["w_down"], p["b_down"], nb)
    y = _combine(_blocked(dest, TM_COMBINE), g4.T, x1, p["norm_final"], ybuf)
    return y.reshape(B, T, D)


def kernel(x_prompt, x_sample, norm_mix, w_in, decay_logit, na_rpb, w_ret_out, w_na_out, w_o, norm_ffn,
           w_router, b_router, w_gate_up, b_gate_up, w_down, b_down, norm_final):
    T = x_prompt.shape[1]
    assert x_sample.shape[1] == T and norm_mix.shape[0] == 1
    half = RET_DK // 2
    inv = 1.0 / (ROPE_BASE ** (jnp.arange(half, dtype=F32) * 2.0 / RET_DK))
    ang = jnp.arange(T, dtype=F32)[:, None] * inv[None, :]
    cos, sin = jnp.cos(ang), jnp.sin(ang)
    wr_t = w_router[0].T
    wr_hi = wr_t.astype(BF16)
    p = dict(
        norm_mix=norm_mix[0].reshape(1, D_MODEL),
        w_in=w_in[0].astype(BF16),
        cos=jnp.concatenate([cos, cos], axis=1),
        sin=jnp.concatenate([-sin, sin], axis=1),
        log_g=jax.nn.log_sigmoid(decay_logit[0].astype(F32)),
        na_bias=_na_bias_table(na_rpb[0]),
        w_ret_out=w_ret_out[0].astype(BF16),
        w_na_out=w_na_out[0].astype(BF16),
        w_o=w_o[0].astype(BF16),
        norm_ffn=norm_ffn[0].reshape(1, D_MODEL),
        wr_hi=wr_hi,
        wr_lo=(wr_t - wr_hi.astype(F32)).astype(BF16),
        b_router=b_router[0].reshape(N_EXPERTS, 1),
        w_gate_up=w_gate_up[0],
        b_gate_up=b_gate_up[0].reshape(N_EXPERTS, 1, 2 * D_EXPERT),
        w_down=w_down[0],
        b_down=b_down[0].reshape(N_EXPERTS, 1, D_MODEL),
        norm_final=norm_final.reshape(1, D_MODEL),
    )
    return (_trunk(x_prompt, p), _trunk(x_sample, p))
```

```python
import functools

import numpy as np
import jax
import jax.numpy as jnp
from jax import lax
from jax.experimental import pallas as pl
from jax.experimental.pallas import tpu as pltpu

F32 = jnp.float32
BF16 = jnp.bfloat16

D_MODEL = 1024
GRID_W = 64
RET_HEADS = 4
RET_DK = 128
RET_CHUNK = 128
ROPE_BASE = 10000.0
NA_HEADS = 8
NA_HD = 64
NA_WIN_ROWS = 8
NA_WIN_COLS = 16
N_EXPERTS = 32
TOP_K = 4
D_EXPERT = 1024
SWIGLU_ALPHA = 1.702
SWIGLU_LIMIT = 7.0
NORM_EPS = 1e-6
RET_W = RET_HEADS * RET_DK
NA_W = NA_HEADS * NA_HD
D_IN = 4 * RET_W + 3 * NA_W + 2 * D_MODEL

LANES = 128
VMEM_LIMIT = 56 * 1024 * 1024
TM_PROJ = 512
TM_DISPATCH = 512
TM_COMBINE = 256
BM_EXPERT = 512
CH_EXPERT = 512
RET_UNROLL = 4
RET_NORM_ROWS = 512
NA_ROWS_PER_STEP = 4
NEG_MASK = -1e30


def _const_spec(shape):
    nd = len(shape)
    return pl.BlockSpec(shape, lambda *_: (0,) * nd, pipeline_mode=pl.Buffered(1))


def _rms(x, g):
    ms = jnp.mean(x * x, axis=-1, keepdims=True)
    return x * lax.rsqrt(ms + NORM_EPS) * g


def _in_proj_kernel(x_ref, g_ref, w_ref, cos_ref, sin_ref,
                    q_ref, k_ref, v_ref, gate_ref, nq_ref, nk_ref, nv_ref, sret_ref, sna_ref):
    h = _rms(x_ref[...], g_ref[...]).astype(BF16)
    cos = cos_ref[...]
    sin = sin_ref[...]

    def proj(lo, width):
        return jnp.dot(h, w_ref[:, lo:lo + width], preferred_element_type=F32)

    def rot(dst, a, scale):
        for hd in range(RET_HEADS):
            blk = a[:, hd * RET_DK:(hd + 1) * RET_DK]
            r = blk * cos + pltpu.roll(blk, RET_DK // 2, 1) * sin
            if scale is not None:
                r = r * scale
            dst[:, hd * RET_DK:(hd + 1) * RET_DK] = r.astype(BF16)

    o = 0
    rot(q_ref, proj(o, RET_W), None); o += RET_W
    rot(k_ref, proj(o, RET_W), RET_DK ** -0.5); o += RET_W
    v_ref[...] = proj(o, RET_W).astype(BF16); o += RET_W
    rg = proj(o, RET_W); o += RET_W
    gate_ref[...] = (rg * jax.nn.sigmoid(rg)).astype(BF16)
    nq_ref[...] = (proj(o, NA_W) * (NA_HD ** -0.5)).astype(BF16); o += NA_W
    nk_ref[...] = proj(o, NA_W).astype(BF16); o += NA_W
    nv_ref[...] = proj(o, NA_W).astype(BF16); o += NA_W
    sret_ref[...] = jax.nn.sigmoid(proj(o, D_MODEL)).astype(BF16); o += D_MODEL
    sna_ref[...] = jax.nn.sigmoid(proj(o, D_MODEL)).astype(BF16)


def _in_proj(x2, g, w_in, cos_t, sin_t, T):
    N = x2.shape[0]
    tm = TM_PROJ
    nt = T // tm
    row = lambda w: pl.BlockSpec((tm, w), lambda i: (i, 0))
    pos = pl.BlockSpec((tm, RET_DK), lambda i: (i % nt, 0))
    half = jax.ShapeDtypeStruct((N, RET_W), BF16)
    full = jax.ShapeDtypeStruct((N, D_MODEL), BF16)
    return pl.pallas_call(
        _in_proj_kernel,
        grid=(N // tm,),
        in_specs=[row(D_MODEL), _const_spec((1, D_MODEL)), _const_spec((D_MODEL, D_IN)), pos, pos],
        out_specs=[row(RET_W)] * 7 + [row(D_MODEL)] * 2,
        out_shape=[half] * 7 + [full] * 2,
        compiler_params=pltpu.CompilerParams(dimension_semantics=("parallel",), vmem_limit_bytes=VMEM_LIMIT),
        name="in_proj",
    )(x2, g, w_in, cos_t, sin_t)


def _retention_kernel(lg_ref, q_ref, k_ref, v_ref, gate_ref, o_ref, accf_ref, accb_ref, *, T):
    C = RET_CHUNK
    nchunk = T // C
    unroll = RET_UNROLL
    hd = pl.program_id(1)
    lgf = lg_ref[0, hd]
    lgb = lg_ref[1, hd]
    ii = lax.broadcasted_iota(jnp.int32, (C, C), 0).astype(F32)
    jj = lax.broadcasted_iota(jnp.int32, (C, C), 1).astype(F32)
    diff = ii - jj
    dmat_f = jnp.where(diff >= 0, jnp.exp(lgf * jnp.maximum(diff, 0.0)), 0.0)
    dmat_b = jnp.where(diff < 0, jnp.exp(lgb * jnp.maximum(-diff, 0.0)), 0.0)
    col = lax.broadcasted_iota(jnp.int32, (C, 1), 0).astype(F32)
    qdec_f = jnp.exp(lgf * (col + 1.0))
    kdec_f = jnp.exp(lgf * (C - 1.0 - col))
    qdec_b = jnp.exp(lgb * (C - col))
    kdec_b = jnp.exp(lgb * col)
    ones = jnp.ones((1, RET_DK), F32)
    cdec_f = jnp.exp(ones * (lgf * C))
    cdec_b = jnp.exp(ones * (lgb * C))
    nt_dims = (((1,), (1,)), ((), ()))
    tn_dims = (((0,), (0,)), ((), ()))

    def chunk(n, state, acc_ref, dmat, qdec, kdec, cdec):
        sl = pl.ds(pl.multiple_of(n * C, C), C)
        q = q_ref[sl, :]
        k = k_ref[sl, :]
        v = v_ref[sl, :]
        s = lax.dot_general(q, k, nt_dims, preferred_element_type=F32) * dmat
        inner = jnp.dot(s.astype(BF16), v, preferred_element_type=F32)
        qd = (q.astype(F32) * qdec).astype(BF16)
        cross = jnp.dot(qd, state.astype(BF16), preferred_element_type=F32)
        kd = (k.astype(F32) * kdec).astype(BF16)
        kv = lax.dot_general(kd, v, tn_dims, preferred_element_type=F32)
        acc_ref[sl, :] = inner + cross
        return cdec * state + kv

    def step(it, carry):
        sf, sb = carry
        for u in range(unroll):
            n = it * unroll + u
            sf = chunk(n, sf, accf_ref, dmat_f, qdec_f, kdec_f, cdec_f)
            sb = chunk(nchunk - 1 - n, sb, accb_ref, dmat_b, qdec_b, kdec_b, cdec_b)
        return sf, sb

    zero = jnp.zeros((RET_DK, RET_DK), F32)
    lax.fori_loop(0, nchunk // unroll, step, (zero, zero))

    rb = RET_NORM_ROWS

    def norm(i, c):
        sl = pl.ds(pl.multiple_of(i * rb, rb), rb)
        o = accf_ref[sl, :] + accb_ref[sl, :]
        mu = jnp.mean(o, axis=-1, keepdims=True)
        d = o - mu
        var = jnp.mean(d * d, axis=-1, keepdims=True)
        o_ref[sl, :] = (gate_ref[sl, :].astype(F32) * (d * lax.rsqrt(var + NORM_EPS))).astype(BF16)
        return c

    lax.fori_loop(0, T // rb, norm, 0)


def _retention(log_g, q, k, v, gate, B, T):
    N = q.shape[0]
    seq = pl.BlockSpec((T, RET_DK), lambda b, h: (b, h))
    return pl.pallas_call(
        functools.partial(_retention_kernel, T=T),
        grid=(B, RET_HEADS),
        in_specs=[pl.BlockSpec(memory_space=pltpu.SMEM), seq, seq, seq, seq],
        out_specs=seq,
        out_shape=jax.ShapeDtypeStruct((N, RET_W), BF16),
        scratch_shapes=[pltpu.VMEM((T, RET_DK), F32), pltpu.VMEM((T, RET_DK), F32)],
        compiler_params=pltpu.CompilerParams(dimension_semantics=("parallel", "parallel"), vmem_limit_bytes=VMEM_LIMIT),
        name="retention",
    )(log_g, q, k, v, gate)


def _na_bias_table(rpb):
    W, WC, R = GRID_W, NA_WIN_COLS, NA_WIN_ROWS
    c = np.arange(W)
    cs = np.clip(c - WC // 2, 0, W - WC)
    kc = np.arange(W)
    valid = (kc[None, :] >= cs[:, None]) & (kc[None, :] < cs[:, None] + WC)
    dc = kc[None, :] - c[:, None] + WC - 1
    onehot = (dc[None] == np.arange(2 * WC - 1)[:, None, None]) & valid[None]
    bc = jnp.einsum("hrd,dck->hrck", rpb.astype(F32), jnp.asarray(onehot, F32), precision=lax.Precision.HIGHEST)
    bc = jnp.where(valid[None, None], bc, NEG_MASK)
    t = jnp.stack([bc[:, d0:d0 + R] for d0 in range(R)], axis=1)
    t = t.transpose(0, 1, 3, 2, 4).reshape(NA_HEADS // 2, 2, R, W, R * W)
    return t.transpose(0, 2, 1, 3, 4).reshape(NA_HEADS // 2, R, 2 * W, R * W)


def _natten_kernel(q_ref, k_ref, v_ref, bias_ref, o_ref, *, T):
    W, R = GRID_W, NA_WIN_ROWS
    rows = T // W
    group = NA_ROWS_PER_STEP
    first = lax.broadcasted_iota(jnp.int32, (1, 2 * NA_HD), 1) < NA_HD
    nt_dims = (((1,), (1,)), ((), ()))

    def one_row(r):
        rs = jnp.clip(r - R // 2, 0, rows - R)
        d0 = rs - r + R - 1
        qs = pl.ds(pl.multiple_of(r * W, W), W)
        ks = pl.ds(pl.multiple_of(rs * W, W), R * W)
        q = q_ref[qs, :]
        zero = jnp.zeros_like(q)
        q2 = jnp.concatenate([jnp.where(first, q, zero), jnp.where(first, zero, q)], axis=0)
        s = lax.dot_general(q2, k_ref[ks, :], nt_dims, preferred_element_type=F32) + bias_ref[d0]
        p = jnp.exp(s - jnp.max(s, axis=-1, keepdims=True))
        l = jnp.sum(p, axis=-1, keepdims=True)
        pv = jnp.dot(p.astype(BF16), v_ref[ks, :], preferred_element_type=F32) / l
        o_ref[qs, :] = jnp.where(first, pv[:W], pv[W:]).astype(BF16)

    def body(it, carry):
        for u in range(group):
            one_row(it * group + u)
        return carry

    lax.fori_loop(0, rows // group, body, 0)


def _natten(nq, nk, nv, bias_tab, B, T):
    N = nq.shape[0]
    seq = pl.BlockSpec((T, 2 * NA_HD), lambda b, h: (b, h))
    bias = pl.BlockSpec((None, NA_WIN_ROWS, 2 * GRID_W, NA_WIN_ROWS * GRID_W), lambda b, h: (h, 0, 0, 0))
    return pl.pallas_call(
        functools.partial(_natten_kernel, T=T),
        grid=(B, NA_HEADS // 2),
        in_specs=[seq, seq, seq, bias],
        out_specs=seq,
        out_shape=jax.ShapeDtypeStruct((N, NA_W), BF16),
        compiler_params=pltpu.CompilerParams(dimension_semantics=("parallel", "parallel"), vmem_limit_bytes=VMEM_LIMIT),
        name="natten",
    )(nq, nk, nv, bias_tab)


def _post_mixer_kernel(ret_ref, na_ref, sret_ref, sna_ref, x_ref, wret_ref, wna_ref, wo_ref, g_ref,
                       wrh_ref, wrl_ref, br_ref,
                       x1_ref, h2_ref, e_ref, gates_ref, rank_ref, cnt_ref, carry_ref):
    tm = x_ref.shape[0]

    @pl.when(pl.program_id(0) == 0)
    def _():
        carry_ref[...] = jnp.zeros_like(carry_ref)

    ret_out = jnp.dot(ret_ref[...], wret_ref[...], preferred_element_type=F32)
    na_out = jnp.dot(na_ref[...], wna_ref[...], preferred_element_type=F32)
    merged = sret_ref[...].astype(F32) * ret_out + sna_ref[...].astype(F32) * na_out
    x1 = x_ref[...] + jnp.dot(merged.astype(BF16), wo_ref[...], preferred_element_type=F32)
    x1_ref[...] = x1
    h2 = _rms(x1, g_ref[...])
    h2_ref[...] = h2

    nt_dims = (((1,), (1,)), ((), ()))
    h_hi = h2.astype(BF16)
    h_lo = (h2 - h_hi.astype(F32)).astype(BF16)
    logits = (lax.dot_general(wrh_ref[...], h_hi, nt_dims, preferred_element_type=F32)
              + lax.dot_general(wrl_ref[...], h_hi, nt_dims, preferred_element_type=F32)
              + lax.dot_general(wrh_ref[...], h_lo, nt_dims, preferred_element_type=F32)
              + br_ref[...])

    eid = lax.broadcasted_iota(jnp.int32, (N_EXPERTS, tm), 0).astype(F32)
    work = logits
    sel, val = [], []
    for _ in range(TOP_K):
        m = jnp.max(work, axis=0, keepdims=True)
        idx = jnp.min(jnp.where(work == m, eid, float(N_EXPERTS)), axis=0, keepdims=True)
        sel.append(idx)
        val.append(m)
        work = jnp.where(eid == idx, -jnp.inf, work)
    ex = [jnp.exp(v - val[0]) for v in val]
    den = ex[0] + ex[1] + ex[2] + ex[3]

    hot = [(eid == s) for s in sel]
    onehot = jnp.where(hot[0] | hot[1] | hot[2] | hot[3], 1.0, 0.0)
    tri = jnp.where(lax.broadcasted_iota(jnp.int32, (tm, tm), 0) < lax.broadcasted_iota(jnp.int32, (tm, tm), 1),
                    1.0, 0.0)
    prefix = jnp.dot(onehot.astype(BF16), tri.astype(BF16), preferred_element_type=F32)
    base = prefix + carry_ref[:, 0:1]
    for k in range(TOP_K):
        e_ref[k:k + 1, :] = sel[k].astype(jnp.int32)
        gates_ref[k:k + 1, :] = ex[k] / den
        rank_ref[k:k + 1, :] = jnp.sum(jnp.where(hot[k], base, 0.0), axis=0, keepdims=True).astype(jnp.int32)
    carry_ref[...] = carry_ref[...] + jnp.sum(onehot, axis=1, keepdims=True)
    cnt_ref[...] = carry_ref[...].astype(jnp.int32)


def _post_mixer(ret, na, sret, sna, x2, wret, wna, wo, g, wr_hi, wr_lo, br):
    N = x2.shape[0]
    tm = TM_PROJ
    row = lambda w: pl.BlockSpec((tm, w), lambda i: (i, 0))
    tok = pl.BlockSpec((TOP_K, tm), lambda i: (0, i))
    return pl.pallas_call(
        _post_mixer_kernel,
        grid=(N // tm,),
        in_specs=[row(RET_W), row(NA_W), row(D_MODEL), row(D_MODEL), row(D_MODEL),
                  _const_spec((RET_W, D_MODEL)), _const_spec((NA_W, D_MODEL)), _const_spec((D_MODEL, D_MODEL)),
                  _const_spec((1, D_MODEL)), _const_spec((N_EXPERTS, D_MODEL)), _const_spec((N_EXPERTS, D_MODEL)),
                  _const_spec((N_EXPERTS, 1))],
        out_specs=[row(D_MODEL), row(D_MODEL), tok, tok, tok,
                   pl.BlockSpec((N_EXPERTS, LANES), lambda i: (0, 0))],
        out_shape=[jax.ShapeDtypeStruct((N, D_MODEL), F32), jax.ShapeDtypeStruct((N, D_MODEL), F32),
                   jax.ShapeDtypeStruct((TOP_K, N), jnp.int32), jax.ShapeDtypeStruct((TOP_K, N), F32),
                   jax.ShapeDtypeStruct((TOP_K, N), jnp.int32),
                   jax.ShapeDtypeStruct((N_EXPERTS, LANES), jnp.int32)],
        scratch_shapes=[pltpu.VMEM((N_EXPERTS, LANES), F32)],
        compiler_params=pltpu.CompilerParams(dimension_semantics=("arbitrary",), vmem_limit_bytes=VMEM_LIMIT),
        name="post_mixer",
    )(ret, na, sret, sna, x2, wret, wna, wo, g, wr_hi, wr_lo, br)


def _row_copy(src, s, dst, d, sem):
    return pltpu.make_async_copy(src.at[pl.ds(s, 1), :], dst.at[pl.ds(d, 1), :], sem)


def _dispatch_kernel(zstart_ref, znum_ref, dest_ref, h_ref, xbuf_ref, zero_ref, sem, zsem):
    tm = h_ref.shape[0]

    @pl.when(pl.program_id(0) == 0)
    def _():
        zero_ref[...] = jnp.zeros_like(zero_ref)

        def per_expert(e, c):
            base = zstart_ref[e]

            def fill(j, c2):
                _row_copy(zero_ref, 0, xbuf_ref, base + j, zsem).start()
                return c2

            lax.fori_loop(0, znum_ref[e], fill, 0)

            def drain(j, c2):
                _row_copy(zero_ref, 0, xbuf_ref, base + j, zsem).wait()
                return c2

            lax.fori_loop(0, znum_ref[e], drain, 0)
            return c

        lax.fori_loop(0, N_EXPERTS, per_expert, 0)

    def issue(j, c):
        for k in range(TOP_K):
            _row_copy(h_ref, j, xbuf_ref, dest_ref[0, k * tm + j], sem).start(priority=k % 2)
        return c

    lax.fori_loop(0, tm, issue, 0)

    def drain(j, c):
        for k in range(TOP_K):
            _row_copy(h_ref, j, xbuf_ref, dest_ref[0, k * tm + j], sem).wait()
        return c

    lax.fori_loop(0, tm, drain, 0)


def _dispatch(zstart, znum, dest_blk, h2, P):
    N = h2.shape[0]
    tm = TM_DISPATCH
    grid_spec = pltpu.PrefetchScalarGridSpec(
        num_scalar_prefetch=2,
        grid=(N // tm,),
        in_specs=[pl.BlockSpec((None, 1, TOP_K * tm), lambda i, *_: (i, 0, 0), memory_space=pltpu.SMEM),
                  pl.BlockSpec((tm, D_MODEL), lambda i, *_: (i, 0))],
        out_specs=pl.BlockSpec(memory_space=pl.ANY),
        scratch_shapes=[pltpu.VMEM((8, D_MODEL), F32), pltpu.SemaphoreType.DMA, pltpu.SemaphoreType.DMA],
    )
    return pl.pallas_call(
        _dispatch_kernel,
        grid_spec=grid_spec,
        out_shape=jax.ShapeDtypeStruct((P, D_MODEL), F32),
        compiler_params=pltpu.CompilerParams(dimension_semantics=("arbitrary",), vmem_limit_bytes=VMEM_LIMIT),
        name="dispatch",
    )(zstart, znum, dest_blk, h2)


def _experts_kernel(be_ref, nu_ref, x_ref, wgu_ref, bgu_ref, wdn_ref, bdn_ref, y_ref, wgu_bf, wdn_bf):
    b = pl.program_id(0)

    @pl.when(b < nu_ref[0])
    def _():
        prev = be_ref[jnp.maximum(b - 1, 0)]

        @pl.when((b == 0) | (be_ref[b] != prev))
        def _():
            wgu_bf[...] = wgu_ref[...].astype(BF16)
            wdn_bf[...] = wdn_ref[...].astype(BF16)

        x = x_ref[...].astype(BF16)
        acc = jnp.zeros(y_ref.shape, F32) + bdn_ref[...]
        for c in range(0, D_EXPERT, CH_EXPERT):
            gate = jnp.dot(x, wgu_bf[:, c:c + CH_EXPERT], preferred_element_type=F32) + bgu_ref[:, c:c + CH_EXPERT]
            up = (jnp.dot(x, wgu_bf[:, D_EXPERT + c:D_EXPERT + c + CH_EXPERT], preferred_element_type=F32)
                  + bgu_ref[:, D_EXPERT + c:D_EXPERT + c + CH_EXPERT])
            gate = jnp.minimum(gate, SWIGLU_LIMIT)
            up = jnp.clip(up, -SWIGLU_LIMIT, SWIGLU_LIMIT)
            act = (up + 1.0) * (gate * jax.nn.sigmoid(SWIGLU_ALPHA * gate))
            acc = acc + jnp.dot(act.astype(BF16), wdn_bf[c:c + CH_EXPERT, :], preferred_element_type=F32)
        y_ref[...] = acc


def _experts(blk_e, n_used, xbuf, wgu, bgu, wdn, bdn, nb):
    bm = BM_EXPERT
    P = xbuf.shape[0]
    used = lambda b, be, nu: jnp.minimum(b, nu[0] - 1)
    grid_spec = pltpu.PrefetchScalarGridSpec(
        num_scalar_prefetch=2,
        grid=(nb,),
        in_specs=[pl.BlockSpec((bm, D_MODEL), lambda b, be, nu: (used(b, be, nu), 0)),
                  pl.BlockSpec((None, D_MODEL, 2 * D_EXPERT), lambda b, be, nu: (be[b], 0, 0)),
                  pl.BlockSpec((None, 1, 2 * D_EXPERT), lambda b, be, nu: (be[b], 0, 0)),
                  pl.BlockSpec((None, D_EXPERT, D_MODEL), lambda b, be, nu: (be[b], 0, 0)),
                  pl.BlockSpec((None, 1, D_MODEL), lambda b, be, nu: (be[b], 0, 0))],
        out_specs=pl.BlockSpec((bm, D_MODEL), lambda b, be, nu: (used(b, be, nu), 0)),
        scratch_shapes=[pltpu.VMEM((D_MODEL, 2 * D_EXPERT), BF16), pltpu.VMEM((D_EXPERT, D_MODEL), BF16)],
    )
    return pl.pallas_call(
        _experts_kernel,
        grid_spec=grid_spec,
        out_shape=jax.ShapeDtypeStruct((P, D_MODEL), F32),
        compiler_params=pltpu.CompilerParams(dimension_semantics=("arbitrary",), vmem_limit_bytes=VMEM_LIMIT),
        name="experts",
    )(blk_e, n_used, xbuf, wgu, bgu, wdn, bdn)


def _combine_kernel(dcur_ref, dnxt_ref, gates_ref, x1_ref, g_ref, ybuf_ref, o_ref, gbuf, sems):
    tm = x1_ref.shape[0]
    i = pl.program_id(0)
    n = pl.num_programs(0)
    slot = lax.rem(i, 2)

    def gather(dref, s, start):
        def body(j, c):
            for k in range(TOP_K):
                cp = pltpu.make_async_copy(ybuf_ref.at[pl.ds(dref[0, k * tm + j], 1), :],
                                           gbuf.at[s, k, pl.ds(j, 1), :], sems.at[s])
                if start:
                    cp.start(priority=k % 2)
                else:
                    cp.wait()
            return c

        lax.fori_loop(0, tm, body, 0)

    @pl.when(i == 0)
    def _():
        gather(dcur_ref, 0, True)

    @pl.when(i + 1 < n)
    def _():
        gather(dnxt_ref, 1 - slot, True)

    gather(dcur_ref, slot, False)

    out = x1_ref[...]
    gt = gates_ref[...]
    for k in range(TOP_K):
        out = out + gt[:, k:k + 1] * gbuf[slot, k]
    o_ref[...] = _rms(out, g_ref[...])


def _combine(dest_blk, gates_t, x1, g, ybuf):
    N = x1.shape[0]
    tm = TM_COMBINE
    nblk = N // tm
    dspec = lambda f: pl.BlockSpec((None, 1, TOP_K * tm), f, memory_space=pltpu.SMEM)
    return pl.pallas_call(
        _combine_kernel,
        grid=(nblk,),
        in_specs=[dspec(lambda i: (i, 0, 0)), dspec(lambda i: (jnp.minimum(i + 1, nblk - 1), 0, 0)),
                  pl.BlockSpec((tm, TOP_K), lambda i: (i, 0)),
                  pl.BlockSpec((tm, D_MODEL), lambda i: (i, 0)),
                  _const_spec((1, D_MODEL)),
                  pl.BlockSpec(memory_space=pl.ANY)],
        out_specs=pl.BlockSpec((tm, D_MODEL), lambda i: (i, 0)),
        out_shape=jax.ShapeDtypeStruct((N, D_MODEL), F32),
        scratch_shapes=[pltpu.VMEM((2, TOP_K, tm, D_MODEL), F32), pltpu.SemaphoreType.DMA((2,))],
        compiler_params=pltpu.CompilerParams(dimension_semantics=("arbitrary",), vmem_limit_bytes=VMEM_LIMIT),
        name="combine",
    )(dest_blk, dest_blk, gates_t, x1, g, ybuf)


def _blocked(dest, tm):
    K, N = dest.shape
    return dest.reshape(K, N // tm, tm).transpose(1, 0, 2).reshape(N // tm, 1, K * tm)


def _trunk(x, p):
    B, T, D = x.shape
    N = B * T
    x2 = x.reshape(N, D)
    q, k, v, gate, nq, nk, nv, sret, sna = _in_proj(x2, p["norm_mix"], p["w_in"], p["cos"], p["sin"], T)
    ret = _retention(p["log_g"], q, k, v, gate, B, T)
    na = _natten(nq, nk, nv, p["na_bias"], B, T)
    x1, h2, e4, g4, r4, cnt = _post_mixer(ret, na, sret, sna, x2, p["w_ret_out"], p["w_na_out"], p["w_o"],
                                          p["norm_ffn"], p["wr_hi"], p["wr_lo"], p["b_router"])
    bm = BM_EXPERT
    counts = cnt[:, 0]
    padded = (counts + bm - 1) // bm * bm
    pend = jnp.cumsum(padded)
    pstart = pend - padded
    eids = jnp.arange(N_EXPERTS, dtype=jnp.int32)
    dest = r4 + jnp.sum(jnp.where(e4[None] == eids[:, None, None], pstart[:, None, None], 0), axis=0)
    nb = (N * TOP_K) // bm + N_EXPERTS
    n_used = (pend[-1] // bm).astype(jnp.int32).reshape(1)
    blk = jnp.minimum(jnp.arange(nb, dtype=jnp.int32), n_used[0] - 1)
    blk_e = jnp.minimum(jnp.sum((pend[None, :] <= (blk * bm)[:, None]).astype(jnp.int32), axis=1), N_EXPERTS - 1)
    xbuf = _dispatch((pstart + counts).astype(jnp.int32), (padded - counts).astype(jnp.int32),
                     _blocked(dest, TM_DISPATCH), h2, nb * bm)
    ybuf = _experts(blk_e, n_used, xbuf, p["w_gate_up"], p["b_gate_up"], p["w_down"], p["b_down"], nb)
    y = _combine(_blocked(dest, TM_COMBINE), g4.T, x1, p["norm_final"], ybuf)
    return y.reshape(B, T, D)


def kernel(x_prompt, x_sample, norm_mix, w_in, decay_logit, na_rpb, w_ret_out, w_na_out, w_o, norm_ffn,
           w_router, b_router, w_gate_up, b_gate_up, w_down, b_down, norm_final):
    T = x_prompt.shape[1]
    assert x_sample.shape[1] == T and norm_mix.shape[0] == 1
    half = RET_DK // 2
    inv = 1.0 / (ROPE_BASE ** (jnp.arange(half, dtype=F32) * 2.0 / RET_DK))
    ang = jnp.arange(T, dtype=F32)[:, None] * inv[None, :]
    cos, sin = jnp.cos(ang), jnp.sin(ang)
    wr_t = w_router[0].T
    wr_hi = wr_t.astype(BF16)
    p = dict(
        norm_mix=norm_mix[0].reshape(1, D_MODEL),
        w_in=w_in[0].astype(BF16),
        cos=jnp.concatenate([cos, cos], axis=1),
        sin=jnp.concatenate([-sin, sin], axis=1),
        log_g=jax.nn.log_sigmoid(decay_logit[0].astype(F32)),
        na_bias=_na_bias_table(na_rpb[0]),
        w_ret_out=w_ret_out[0].astype(BF16),
        w_na_out=w_na_out[0].astype(BF16),
        w_o=w_o[0].astype(BF16),
        norm_ffn=norm_ffn[0].reshape(1, D_MODEL),
        wr_hi=wr_hi,
        wr_lo=(wr_t - wr_hi.astype(F32)).astype(BF16),
        b_router=b_router[0].reshape(N_EXPERTS, 1),
        w_gate_up=w_gate_up[0],
        b_gate_up=b_gate_up[0].reshape(N_EXPERTS, 1, 2 * D_EXPERT),
        w_down=w_down[0],
        b_down=b_down[0].reshape(N_EXPERTS, 1, D_MODEL),
        norm_final=norm_final.reshape(1, D_MODEL),
    )
    return (_trunk(x_prompt, p), _trunk(x_sample, p))
```

```python
import functools

import numpy as np
import jax
import jax.numpy as jnp
from jax import lax
from jax.experimental import pallas as pl
from jax.experimental.pallas import tpu as pltpu

F32 = jnp.float32
BF16 = jnp.bfloat16

D_MODEL = 1024
GRID_W = 64
RET_HEADS = 4
RET_DK = 128
RET_CHUNK = 128
ROPE_BASE = 10000.0
NA_HEADS = 8
NA_HD = 64
NA_WIN_ROWS = 8
NA_WIN_COLS = 16
N_EXPERTS = 32
TOP_K = 4
D_EXPERT = 1024
SWIGLU_ALPHA = 1.702
SWIGLU_LIMIT = 7.0
NORM_EPS = 1e-6
RET_W = RET_HEADS * RET_DK
NA_W = NA_HEADS * NA_HD
D_IN = 4 * RET_W + 3 * NA_W + 2 * D_MODEL

LANES = 128
SUBLANES = 8
TILE_CHUNKS = D_MODEL // LANES
assert TILE_CHUNKS == SUBLANES
VMEM_LIMIT = 56 * 1024 * 1024
TM_PROJ = 512
TM_DISPATCH = 512
TM_COMBINE = 256
BM_EXPERT = 512
CH_EXPERT = 512
RET_UNROLL = 4
RET_NORM_ROWS = 512
NA_ROWS_PER_STEP = 4
NEG_MASK = -1e30


def _const_spec(shape):
    nd = len(shape)
    return pl.BlockSpec(shape, lambda *_: (0,) * nd, pipeline_mode=pl.Buffered(1))


def _rms(x, g):
    ms = jnp.mean(x * x, axis=-1, keepdims=True)
    return x * lax.rsqrt(ms + NORM_EPS) * g


def _store_token_tiles(ref, x):
    rows = x.shape[0]
    for c in range(TILE_CHUNKS):
        ref[pl.ds(c, rows, stride=TILE_CHUNKS), :] = x[:, c * LANES:(c + 1) * LANES]


def _load_token_tiles(ref, rows):
    return jnp.concatenate([ref[pl.ds(c, rows, stride=TILE_CHUNKS), :] for c in range(TILE_CHUNKS)], axis=1)


def _in_proj_kernel(x_ref, g_ref, w_ref, cos_ref, sin_ref,
                    q_ref, k_ref, v_ref, gate_ref, nq_ref, nk_ref, nv_ref, sret_ref, sna_ref):
    h = _rms(x_ref[...], g_ref[...]).astype(BF16)
    cos = cos_ref[...]
    sin = sin_ref[...]

    def proj(lo, width):
        return jnp.dot(h, w_ref[:, lo:lo + width], preferred_element_type=F32)

    def rot(dst, a, scale):
        for hd in range(RET_HEADS):
            blk = a[:, hd * RET_DK:(hd + 1) * RET_DK]
            r = blk * cos + pltpu.roll(blk, RET_DK // 2, 1) * sin
            if scale is not None:
                r = r * scale
            dst[:, hd * RET_DK:(hd + 1) * RET_DK] = r.astype(BF16)

    o = 0
    rot(q_ref, proj(o, RET_W), None); o += RET_W
    rot(k_ref, proj(o, RET_W), RET_DK ** -0.5); o += RET_W
    v_ref[...] = proj(o, RET_W).astype(BF16); o += RET_W
    rg = proj(o, RET_W); o += RET_W
    gate_ref[...] = (rg * jax.nn.sigmoid(rg)).astype(BF16)
    nq_ref[...] = (proj(o, NA_W) * (NA_HD ** -0.5)).astype(BF16); o += NA_W
    nk_ref[...] = proj(o, NA_W).astype(BF16); o += NA_W
    nv_ref[...] = proj(o, NA_W).astype(BF16); o += NA_W
    sret_ref[...] = jax.nn.sigmoid(proj(o, D_MODEL)).astype(BF16); o += D_MODEL
    sna_ref[...] = jax.nn.sigmoid(proj(o, D_MODEL)).astype(BF16)


def _in_proj(x2, g, w_in, cos_t, sin_t, T):
    N = x2.shape[0]
    tm = TM_PROJ
    nt = T // tm
    row = lambda w: pl.BlockSpec((tm, w), lambda i: (i, 0))
    pos = pl.BlockSpec((tm, RET_DK), lambda i: (i % nt, 0))
    half = jax.ShapeDtypeStruct((N, RET_W), BF16)
    full = jax.ShapeDtypeStruct((N, D_MODEL), BF16)
    return pl.pallas_call(
        _in_proj_kernel,
        grid=(N // tm,),
        in_specs=[row(D_MODEL), _const_spec((1, D_MODEL)), _const_spec((D_MODEL, D_IN)), pos, pos],
        out_specs=[row(RET_W)] * 7 + [row(D_MODEL)] * 2,
        out_shape=[half] * 7 + [full] * 2,
        compiler_params=pltpu.CompilerParams(dimension_semantics=("parallel",), vmem_limit_bytes=VMEM_LIMIT),
        name="in_proj",
    )(x2, g, w_in, cos_t, sin_t)


def _retention_kernel(lg_ref, q_ref, k_ref, v_ref, gate_ref, o_ref, accf_ref, accb_ref, *, T):
    C = RET_CHUNK
    nchunk = T // C
    unroll = RET_UNROLL
    hd = pl.program_id(1)
    lgf = lg_ref[0, hd]
    lgb = lg_ref[1, hd]
    ii = lax.broadcasted_iota(jnp.int32, (C, C), 0).astype(F32)
    jj = lax.broadcasted_iota(jnp.int32, (C, C), 1).astype(F32)
    diff = ii - jj
    dmat_f = jnp.where(diff >= 0, jnp.exp(lgf * jnp.maximum(diff, 0.0)), 0.0)
    dmat_b = jnp.where(diff < 0, jnp.exp(lgb * jnp.maximum(-diff, 0.0)), 0.0)
    col = lax.broadcasted_iota(jnp.int32, (C, 1), 0).astype(F32)
    qdec_f = jnp.exp(lgf * (col + 1.0))
    kdec_f = jnp.exp(lgf * (C - 1.0 - col))
    qdec_b = jnp.exp(lgb * (C - col))
    kdec_b = jnp.exp(lgb * col)
    ones = jnp.ones((1, RET_DK), F32)
    cdec_f = jnp.exp(ones * (lgf * C))
    cdec_b = jnp.exp(ones * (lgb * C))
    nt_dims = (((1,), (1,)), ((), ()))
    tn_dims = (((0,), (0,)), ((), ()))

    def chunk(n, state, acc_ref, dmat, qdec, kdec, cdec):
        sl = pl.ds(pl.multiple_of(n * C, C), C)
        q = q_ref[sl, :]
        k = k_ref[sl, :]
        v = v_ref[sl, :]
        s = lax.dot_general(q, k, nt_dims, preferred_element_type=F32) * dmat
        inner = jnp.dot(s.astype(BF16), v, preferred_element_type=F32)
        qd = (q.astype(F32) * qdec).astype(BF16)
        cross = jnp.dot(qd, state.astype(BF16), preferred_element_type=F32)
        kd = (k.astype(F32) * kdec).astype(BF16)
        kv = lax.dot_general(kd, v, tn_dims, preferred_element_type=F32)
        acc_ref[sl, :] = inner + cross
        return cdec * state + kv

    def step(it, carry):
        sf, sb = carry
        for u in range(unroll):
            n = it * unroll + u
            sf = chunk(n, sf, accf_ref, dmat_f, qdec_f, kdec_f, cdec_f)
            sb = chunk(nchunk - 1 - n, sb, accb_ref, dmat_b, qdec_b, kdec_b, cdec_b)
        return sf, sb

    zero = jnp.zeros((RET_DK, RET_DK), F32)
    lax.fori_loop(0, nchunk // unroll, step, (zero, zero))

    rb = RET_NORM_ROWS

    def norm(i, c):
        sl = pl.ds(pl.multiple_of(i * rb, rb), rb)
        o = accf_ref[sl, :] + accb_ref[sl, :]
        mu = jnp.mean(o, axis=-1, keepdims=True)
        d = o - mu
        var = jnp.mean(d * d, axis=-1, keepdims=True)
        o_ref[sl, :] = (gate_ref[sl, :].astype(F32) * (d * lax.rsqrt(var + NORM_EPS))).astype(BF16)
        return c

    lax.fori_loop(0, T // rb, norm, 0)


def _retention(log_g, q, k, v, gate, B, T):
    N = q.shape[0]
    seq = pl.BlockSpec((T, RET_DK), lambda b, h: (b, h))
    return pl.pallas_call(
        functools.partial(_retention_kernel, T=T),
        grid=(B, RET_HEADS),
        in_specs=[pl.BlockSpec(memory_space=pltpu.SMEM), seq, seq, seq, seq],
        out_specs=seq,
        out_shape=jax.ShapeDtypeStruct((N, RET_W), BF16),
        scratch_shapes=[pltpu.VMEM((T, RET_DK), F32), pltpu.VMEM((T, RET_DK), F32)],
        compiler_params=pltpu.CompilerParams(dimension_semantics=("parallel", "parallel"), vmem_limit_bytes=VMEM_LIMIT),
        name="retention",
    )(log_g, q, k, v, gate)


def _na_bias_table(rpb):
    W, WC, R = GRID_W, NA_WIN_COLS, NA_WIN_ROWS
    c = np.arange(W)
    cs = np.clip(c - WC // 2, 0, W - WC)
    kc = np.arange(W)
    valid = (kc[None, :] >= cs[:, None]) & (kc[None, :] < cs[:, None] + WC)
    dc = kc[None, :] - c[:, None] + WC - 1
    onehot = (dc[None] == np.arange(2 * WC - 1)[:, None, None]) & valid[None]
    bc = jnp.einsum("hrd,dck->hrck", rpb.astype(F32), jnp.asarray(onehot, F32), precision=lax.Precision.HIGHEST)
    bc = jnp.where(valid[None, None], bc, NEG_MASK)
    t = jnp.stack([bc[:, d0:d0 + R] for d0 in range(R)], axis=1)
    t = t.transpose(0, 1, 3, 2, 4).reshape(NA_HEADS // 2, 2, R, W, R * W)
    return t.transpose(0, 2, 1, 3, 4).reshape(NA_HEADS // 2, R, 2 * W, R * W)


def _natten_kernel(q_ref, k_ref, v_ref, bias_ref, o_ref, *, T):
    W, R = GRID_W, NA_WIN_ROWS
    rows = T // W
    group = NA_ROWS_PER_STEP
    first = lax.broadcasted_iota(jnp.int32, (1, 2 * NA_HD), 1) < NA_HD
    nt_dims = (((1,), (1,)), ((), ()))

    def one_row(r):
        rs = jnp.clip(r - R // 2, 0, rows - R)
        d0 = rs - r + R - 1
        qs = pl.ds(pl.multiple_of(r * W, W), W)
        ks = pl.ds(pl.multiple_of(rs * W, W), R * W)
        q = q_ref[qs, :]
        zero = jnp.zeros_like(q)
        q2 = jnp.concatenate([jnp.where(first, q, zero), jnp.where(first, zero, q)], axis=0)
        s = lax.dot_general(q2, k_ref[ks, :], nt_dims, preferred_element_type=F32) + bias_ref[d0]
        p = jnp.exp(s - jnp.max(s, axis=-1, keepdims=True))
        l = jnp.sum(p, axis=-1, keepdims=True)
        pv = jnp.dot(p.astype(BF16), v_ref[ks, :], preferred_element_type=F32) / l
        o_ref[qs, :] = jnp.where(first, pv[:W], pv[W:]).astype(BF16)

    def body(it, carry):
        for u in range(group):
            one_row(it * group + u)
        return carry

    lax.fori_loop(0, rows // group, body, 0)


def _natten(nq, nk, nv, bias_tab, B, T):
    N = nq.shape[0]
    seq = pl.BlockSpec((T, 2 * NA_HD), lambda b, h: (b, h))
    bias = pl.BlockSpec((None, NA_WIN_ROWS, 2 * GRID_W, NA_WIN_ROWS * GRID_W), lambda b, h: (h, 0, 0, 0))
    return pl.pallas_call(
        functools.partial(_natten_kernel, T=T),
        grid=(B, NA_HEADS // 2),
        in_specs=[seq, seq, seq, bias],
        out_specs=seq,
        out_shape=jax.ShapeDtypeStruct((N, NA_W), BF16),
        compiler_params=pltpu.CompilerParams(dimension_semantics=("parallel", "parallel"), vmem_limit_bytes=VMEM_LIMIT),
        name="natten",
    )(nq, nk, nv, bias_tab)


def _post_mixer_kernel(ret_ref, na_ref, sret_ref, sna_ref, x_ref, wret_ref, wna_ref, wo_ref, g_ref,
                       wrh_ref, wrl_ref, br_ref,
                       x1_ref, h2_ref, e_ref, gates_ref, rank_ref, cnt_ref, carry_ref):
    tm = x_ref.shape[0]

    @pl.when(pl.program_id(0) == 0)
    def _():
        carry_ref[...] = jnp.zeros_like(carry_ref)

    ret_out = jnp.dot(ret_ref[...], wret_ref[...], preferred_element_type=F32)
    na_out = jnp.dot(na_ref[...], wna_ref[...], preferred_element_type=F32)
    merged = sret_ref[...].astype(F32) * ret_out + sna_ref[...].astype(F32) * na_out
    x1 = x_ref[...] + jnp.dot(merged.astype(BF16), wo_ref[...], preferred_element_type=F32)
    x1_ref[...] = x1
    h2 = _rms(x1, g_ref[...])
    _store_token_tiles(h2_ref, h2)

    nt_dims = (((1,), (1,)), ((), ()))
    h_hi = h2.astype(BF16)
    h_lo = (h2 - h_hi.astype(F32)).astype(BF16)
    logits = (lax.dot_general(wrh_ref[...], h_hi, nt_dims, preferred_element_type=F32)
              + lax.dot_general(wrl_ref[...], h_hi, nt_dims, preferred_element_type=F32)
              + lax.dot_general(wrh_ref[...], h_lo, nt_dims, preferred_element_type=F32)
              + br_ref[...])

    eid = lax.broadcasted_iota(jnp.int32, (N_EXPERTS, tm), 0).astype(F32)
    work = logits
    sel, val = [], []
    for _ in range(TOP_K):
        m = jnp.max(work, axis=0, keepdims=True)
        idx = jnp.min(jnp.where(work == m, eid, float(N_EXPERTS)), axis=0, keepdims=True)
        sel.append(idx)
        val.append(m)
        work = jnp.where(eid == idx, -jnp.inf, work)
    ex = [jnp.exp(v - val[0]) for v in val]
    den = ex[0] + ex[1] + ex[2] + ex[3]

    hot = [(eid == s) for s in sel]
    onehot = jnp.where(hot[0] | hot[1] | hot[2] | hot[3], 1.0, 0.0)
    tri = jnp.where(lax.broadcasted_iota(jnp.int32, (tm, tm), 0) < lax.broadcasted_iota(jnp.int32, (tm, tm), 1),
                    1.0, 0.0)
    prefix = jnp.dot(onehot.astype(BF16), tri.astype(BF16), preferred_element_type=F32)
    base = prefix + carry_ref[:, 0:1]
    for k in range(TOP_K):
        e_ref[k:k + 1, :] = sel[k].astype(jnp.int32)
        gates_ref[k:k + 1, :] = ex[k] / den
        rank_ref[k:k + 1, :] = jnp.sum(jnp.where(hot[k], base, 0.0), axis=0, keepdims=True).astype(jnp.int32)
    carry_ref[...] = carry_ref[...] + jnp.sum(onehot, axis=1, keepdims=True)
    cnt_ref[...] = carry_ref[...].astype(jnp.int32)


def _post_mixer(ret, na, sret, sna, x2, wret, wna, wo, g, wr_hi, wr_lo, br):
    N = x2.shape[0]
    tm = TM_PROJ
    row = lambda w: pl.BlockSpec((tm, w), lambda i: (i, 0))
    tok = pl.BlockSpec((TOP_K, tm), lambda i: (0, i))
    return pl.pallas_call(
        _post_mixer_kernel,
        grid=(N // tm,),
        in_specs=[row(RET_W), row(NA_W), row(D_MODEL), row(D_MODEL), row(D_MODEL),
                  _const_spec((RET_W, D_MODEL)), _const_spec((NA_W, D_MODEL)), _const_spec((D_MODEL, D_MODEL)),
                  _const_spec((1, D_MODEL)), _const_spec((N_EXPERTS, D_MODEL)), _const_spec((N_EXPERTS, D_MODEL)),
                  _const_spec((N_EXPERTS, 1))],
        out_specs=[row(D_MODEL), pl.BlockSpec((tm * SUBLANES, LANES), lambda i: (i, 0)), tok, tok, tok,
                   pl.BlockSpec((N_EXPERTS, LANES), lambda i: (0, 0))],
        out_shape=[jax.ShapeDtypeStruct((N, D_MODEL), F32), jax.ShapeDtypeStruct((N * SUBLANES, LANES), F32),
                   jax.ShapeDtypeStruct((TOP_K, N), jnp.int32), jax.ShapeDtypeStruct((TOP_K, N), F32),
                   jax.ShapeDtypeStruct((TOP_K, N), jnp.int32),
                   jax.ShapeDtypeStruct((N_EXPERTS, LANES), jnp.int32)],
        scratch_shapes=[pltpu.VMEM((N_EXPERTS, LANES), F32)],
        compiler_params=pltpu.CompilerParams(dimension_semantics=("arbitrary",), vmem_limit_bytes=VMEM_LIMIT),
        name="post_mixer",
    )(ret, na, sret, sna, x2, wret, wna, wo, g, wr_hi, wr_lo, br)


def _row_copy(src, s, dst, d, sem):
    tile = lambda t: pl.ds(pl.multiple_of(t * SUBLANES, SUBLANES), SUBLANES)
    return pltpu.make_async_copy(src.at[tile(s), :], dst.at[tile(d), :], sem)


def _dispatch_kernel(zstart_ref, znum_ref, dest_ref, h_ref, xbuf_ref, zero_ref, sem, zsem):
    tm = h_ref.shape[0] // SUBLANES

    @pl.when(pl.program_id(0) == 0)
    def _():
        zero_ref[...] = jnp.zeros_like(zero_ref)

        def per_expert(e, c):
            base = zstart_ref[e]

            def fill(j, c2):
                _row_copy(zero_ref, 0, xbuf_ref, base + j, zsem).start()
                return c2

            lax.fori_loop(0, znum_ref[e], fill, 0)

            def drain(j, c2):
                _row_copy(zero_ref, 0, xbuf_ref, base + j, zsem).wait()
                return c2

            lax.fori_loop(0, znum_ref[e], drain, 0)
            return c

        lax.fori_loop(0, N_EXPERTS, per_expert, 0)

    def issue(j, c):
        for k in range(TOP_K):
            _row_copy(h_ref, j, xbuf_ref, dest_ref[0, k * tm + j], sem).start(priority=k % 2)
        return c

    lax.fori_loop(0, tm, issue, 0)

    def drain(j, c):
        for k in range(TOP_K):
            _row_copy(h_ref, j, xbuf_ref, dest_ref[0, k * tm + j], sem).wait()
        return c

    lax.fori_loop(0, tm, drain, 0)


def _dispatch(zstart, znum, dest_blk, h2, P):
    N = h2.shape[0] // SUBLANES
    tm = TM_DISPATCH
    grid_spec = pltpu.PrefetchScalarGridSpec(
        num_scalar_prefetch=2,
        grid=(N // tm,),
        in_specs=[pl.BlockSpec((None, 1, TOP_K * tm), lambda i, *_: (i, 0, 0), memory_space=pltpu.SMEM),
                  pl.BlockSpec((tm * SUBLANES, LANES), lambda i, *_: (i, 0))],
        out_specs=pl.BlockSpec(memory_space=pl.ANY),
        scratch_shapes=[pltpu.VMEM((SUBLANES, LANES), F32), pltpu.SemaphoreType.DMA, pltpu.SemaphoreType.DMA],
    )
    return pl.pallas_call(
        _dispatch_kernel,
        grid_spec=grid_spec,
        out_shape=jax.ShapeDtypeStruct((P * SUBLANES, LANES), F32),
        compiler_params=pltpu.CompilerParams(dimension_semantics=("arbitrary",), vmem_limit_bytes=VMEM_LIMIT),
        name="dispatch",
    )(zstart, znum, dest_blk, h2)


def _experts_kernel(be_ref, nu_ref, x_ref, wgu_ref, bgu_ref, wdn_ref, bdn_ref, y_ref, wgu_bf, wdn_bf):
    b = pl.program_id(0)

    @pl.when(b < nu_ref[0])
    def _():
        prev = be_ref[jnp.maximum(b - 1, 0)]

        @pl.when((b == 0) | (be_ref[b] != prev))
        def _():
            wgu_bf[...] = wgu_ref[...].astype(BF16)
            wdn_bf[...] = wdn_ref[...].astype(BF16)

        bm = x_ref.shape[0] // SUBLANES
        x = _load_token_tiles(x_ref, bm).astype(BF16)
        acc = jnp.zeros((bm, D_MODEL), F32) + bdn_ref[...]
        for c in range(0, D_EXPERT, CH_EXPERT):
            gate = jnp.dot(x, wgu_bf[:, c:c + CH_EXPERT], preferred_element_type=F32) + bgu_ref[:, c:c + CH_EXPERT]
            up = (jnp.dot(x, wgu_bf[:, D_EXPERT + c:D_EXPERT + c + CH_EXPERT], preferred_element_type=F32)
                  + bgu_ref[:, D_EXPERT + c:D_EXPERT + c + CH_EXPERT])
            gate = jnp.minimum(gate, SWIGLU_LIMIT)
            up = jnp.clip(up, -SWIGLU_LIMIT, SWIGLU_LIMIT)
            act = (up + 1.0) * (gate * jax.nn.sigmoid(SWIGLU_ALPHA * gate))
            acc = acc + jnp.dot(act.astype(BF16), wdn_bf[c:c + CH_EXPERT, :], preferred_element_type=F32)
        _store_token_tiles(y_ref, acc)


def _experts(blk_e, n_used, xbuf, wgu, bgu, wdn, bdn, nb):
    bm = BM_EXPERT
    used = lambda b, be, nu: jnp.minimum(b, nu[0] - 1)
    grid_spec = pltpu.PrefetchScalarGridSpec(
        num_scalar_prefetch=2,
        grid=(nb,),
        in_specs=[pl.BlockSpec((bm * SUBLANES, LANES), lambda b, be, nu: (used(b, be, nu), 0)),
                  pl.BlockSpec((None, D_MODEL, 2 * D_EXPERT), lambda b, be, nu: (be[b], 0, 0)),
                  pl.BlockSpec((None, 1, 2 * D_EXPERT), lambda b, be, nu: (be[b], 0, 0)),
                  pl.BlockSpec((None, D_EXPERT, D_MODEL), lambda b, be, nu: (be[b], 0, 0)),
                  pl.BlockSpec((None, 1, D_MODEL), lambda b, be, nu: (be[b], 0, 0))],
        out_specs=pl.BlockSpec((bm * SUBLANES, LANES), lambda b, be, nu: (used(b, be, nu), 0)),
        scratch_shapes=[pltpu.VMEM((D_MODEL, 2 * D_EXPERT), BF16), pltpu.VMEM((D_EXPERT, D_MODEL), BF16)],
    )
    return pl.pallas_call(
        _experts_kernel,
        grid_spec=grid_spec,
        out_shape=jax.ShapeDtypeStruct(xbuf.shape, F32),
        compiler_params=pltpu.CompilerParams(dimension_semantics=("arbitrary",), vmem_limit_bytes=VMEM_LIMIT),
        name="experts",
    )(blk_e, n_used, xbuf, wgu, bgu, wdn, bdn)


def _combine_kernel(dcur_ref, dnxt_ref, gates_ref, x1_ref, g_ref, ybuf_ref, o_ref, gbuf, sems):
    tm = x1_ref.shape[0]
    i = pl.program_id(0)
    n = pl.num_programs(0)
    slot = lax.rem(i, 2)

    def gather(dref, s, start):
        def body(j, c):
            for k in range(TOP_K):
                cp = _row_copy(ybuf_ref, dref[0, k * tm + j], gbuf.at[s, k], j, sems.at[s])
                if start:
                    cp.start(priority=k % 2)
                else:
                    cp.wait()
            return c

        lax.fori_loop(0, tm, body, 0)

    @pl.when(i == 0)
    def _():
        gather(dcur_ref, 0, True)

    @pl.when(i + 1 < n)
    def _():
        gather(dnxt_ref, 1 - slot, True)

    gather(dcur_ref, slot, False)

    out = x1_ref[...]
    gt = gates_ref[...]
    for k in range(TOP_K):
        out = out + gt[:, k:k + 1] * _load_token_tiles(gbuf.at[slot, k], tm)
    o_ref[...] = _rms(out, g_ref[...])


def _combine(dest_blk, gates_t, x1, g, ybuf):
    N = x1.shape[0]
    tm = TM_COMBINE
    nblk = N // tm
    dspec = lambda f: pl.BlockSpec((None, 1, TOP_K * tm), f, memory_space=pltpu.SMEM)
    return pl.pallas_call(
        _combine_kernel,
        grid=(nblk,),
        in_specs=[dspec(lambda i: (i, 0, 0)), dspec(lambda i: (jnp.minimum(i + 1, nblk - 1), 0, 0)),
                  pl.BlockSpec((tm, TOP_K), lambda i: (i, 0)),
                  pl.BlockSpec((tm, D_MODEL), lambda i: (i, 0)),
                  _const_spec((1, D_MODEL)),
                  pl.BlockSpec(memory_space=pl.ANY)],
        out_specs=pl.BlockSpec((tm, D_MODEL), lambda i: (i, 0)),
        out_shape=jax.ShapeDtypeStruct((N, D_MODEL), F32),
        scratch_shapes=[pltpu.VMEM((2, TOP_K, tm * SUBLANES, LANES), F32), pltpu.SemaphoreType.DMA((2,))],
        compiler_params=pltpu.CompilerParams(dimension_semantics=("arbitrary",), vmem_limit_bytes=VMEM_LIMIT),
        name="combine",
    )(dest_blk, dest_blk, gates_t, x1, g, ybuf)


def _blocked(dest, tm):
    K, N = dest.shape
    return dest.reshape(K, N // tm, tm).transpose(1, 0, 2).reshape(N // tm, 1, K * tm)


def _trunk(x, p):
    B, T, D = x.shape
    N = B * T
    x2 = x.reshape(N, D)
    q, k, v, gate, nq, nk, nv, sret, sna = _in_proj(x2, p["norm_mix"], p["w_in"], p["cos"], p["sin"], T)
    ret = _retention(p["log_g"], q, k, v, gate, B, T)
    na = _natten(nq, nk, nv, p["na_bias"], B, T)
    x1, h2, e4, g4, r4, cnt = _post_mixer(ret, na, sret, sna, x2, p["w_ret_out"], p["w_na_out"], p["w_o"],
                                          p["norm_ffn"], p["wr_hi"], p["wr_lo"], p["b_router"])
    bm = BM_EXPERT
    counts = cnt[:, 0]
    padded = (counts + bm - 1) // bm * bm
    pend = jnp.cumsum(padded)
    pstart = pend - padded
    eids = jnp.arange(N_EXPERTS, dtype=jnp.int32)
    dest = r4 + jnp.sum(jnp.where(e4[None] == eids[:, None, None], pstart[:, None, None], 0), axis=0)
    nb = (N * TOP_K) // bm + N_EXPERTS
    n_used = (pend[-1] // bm).astype(jnp.int32).reshape(1)
    blk = jnp.minimum(jnp.arange(nb, dtype=jnp.int32), n_used[0] - 1)
    blk_e = jnp.minimum(jnp.sum((pend[None, :] <= (blk * bm)[:, None]).astype(jnp.int32), axis=1), N_EXPERTS - 1)
    xbuf = _dispatch((pstart + counts).astype(jnp.int32), (padded - counts).astype(jnp.int32),
                     _blocked(dest, TM_DISPATCH), h2, nb * bm)
    ybuf = _experts(blk_e, n_used, xbuf, p["w_gate_up"], p["b_gate_up"], p["w_down"], p["b_down"], nb)
    y = _combine(_blocked(dest, TM_COMBINE), g4.T, x1, p["norm_final"], ybuf)
    return y.reshape(B, T, D)


def kernel(x_prompt, x_sample, norm_mix, w_in, decay_logit, na_rpb, w_ret_out, w_na_out, w_o, norm_ffn,
           w_router, b_router, w_gate_up, b_gate_up, w_down, b_down, norm_final):
    T = x_prompt.shape[1]
    assert x_sample.shape[1] == T and norm_mix.shape[0] == 1
    half = RET_DK // 2
    inv = 1.0 / (ROPE_BASE ** (jnp.arange(half, dtype=F32) * 2.0 / RET_DK))
    ang = jnp.arange(T, dtype=F32)[:, None] * inv[None, :]
    cos, sin = jnp.cos(ang), jnp.sin(ang)
    wr_t = w_router[0].T
    wr_hi = wr_t.astype(BF16)
    p = dict(
        norm_mix=norm_mix[0].reshape(1, D_MODEL),
        w_in=w_in[0].astype(BF16),
        cos=jnp.concatenate([cos, cos], axis=1),
        sin=jnp.concatenate([-sin, sin], axis=1),
        log_g=jax.nn.log_sigmoid(decay_logit[0].astype(F32)),
        na_bias=_na_bias_table(na_rpb[0]),
        w_ret_out=w_ret_out[0].astype(BF16),
        w_na_out=w_na_out[0].astype(BF16),
        w_o=w_o[0].astype(BF16),
        norm_ffn=norm_ffn[0].reshape(1, D_MODEL),
        wr_hi=wr_hi,
        wr_lo=(wr_t - wr_hi.astype(F32)).astype(BF16),
        b_router=b_router[0].reshape(N_EXPERTS, 1),
        w_gate_up=w_gate_up[0],
        b_gate_up=b_gate_up[0].reshape(N_EXPERTS, 1, 2 * D_EXPERT),
        w_down=w_down[0],
        b_down=b_down[0].reshape(N_EXPERTS, 1, D_MODEL),
        norm_final=norm_final.reshape(1, D_MODEL),
    )
    return (_trunk(x_prompt, p), _trunk(x_sample, p))
```
